```python
import jax, jax.numpy as jnp
from jax import lax
import numpy as np

D_MODEL = 4096
BATCH = 2
SEQ = 4096
DEPTH = 2

ATTN_WIDTH = D_MODEL // 2
CONV_CH = D_MODEL - ATTN_WIDTH
HEAD_DIM = 64
N_Q_HEADS = ATTN_WIDTH // HEAD_DIM
GQA_GROUP = 8
N_KV_HEADS = N_Q_HEADS // GQA_GROUP
WINDOW = 128
ROT_DIM = HEAD_DIM // 4
ROPE_THETA = 500000.0
CONV_WIDTH = 31
Q_COLS = N_Q_HEADS * HEAD_DIM
KV_COLS = N_KV_HEADS * HEAD_DIM
IN_COLS = Q_COLS + 2 * KV_COLS + 2 * CONV_CH
FFN_MULT = 256
D_FF = ((8 * D_MODEL + 3 * FFN_MULT - 1) // (3 * FFN_MULT)) * FFN_MULT
N_MOD = 6

kernel_name = 'hymba_swa_sink_conformer_adaln_block'


def rms_norm(x, g, eps=1e-6):
    xf = x.astype(jnp.float32)
    y = xf * lax.rsqrt(jnp.mean(xf * xf, axis=-1, keepdims=True) + eps)
    return (y * g.astype(jnp.float32)).astype(x.dtype)


def layer_norm(x, g, b, eps=1e-5):
    xf = x.astype(jnp.float32)
    mu = jnp.mean(xf, axis=-1, keepdims=True)
    var = jnp.mean(jnp.square(xf - mu), axis=-1, keepdims=True)
    y = (xf - mu) * lax.rsqrt(var + eps)
    return (y * g.astype(jnp.float32) + b.astype(jnp.float32)).astype(x.dtype)


def rotary_tables(positions):
    inv_freq = ROPE_THETA ** (-jnp.arange(0, ROT_DIM, 2, dtype=jnp.float32) / ROT_DIM)
    ang = positions.astype(jnp.float32)[..., None] * inv_freq
    return jnp.cos(ang)[:, :, None, :], jnp.sin(ang)[:, :, None, :]


def apply_partial_rotary(t, cos, sin):
    tf = t.astype(jnp.float32)
    half = ROT_DIM // 2
    t1, t2, rest = tf[..., :half], tf[..., half:ROT_DIM], tf[..., ROT_DIM:]
    out = jnp.concatenate([t1 * cos - t2 * sin, t2 * cos + t1 * sin, rest], axis=-1)
    return out.astype(t.dtype)


def sliding_window_sink_attention(q, k, v, sinks):
    B, S = q.shape[0], q.shape[1]
    nb = S // WINDOW
    qb = q.reshape(B, nb, WINDOW, N_KV_HEADS, GQA_GROUP, HEAD_DIM)

    def band(t):
        tb = t.reshape(B, nb, WINDOW, N_KV_HEADS, HEAD_DIM)
        prev = jnp.pad(tb, ((0, 0), (1, 0), (0, 0), (0, 0), (0, 0)))[:, :-1]
        return jnp.concatenate([prev, tb], axis=2)

    kb, vb = band(k), band(v)
    s = jnp.einsum('bnqhgd,bnkhd->bnhgqk', qb, kb,
                   preferred_element_type=jnp.float32) * (HEAD_DIM ** -0.5)
    qi = jnp.arange(WINDOW)[:, None]
    ki = jnp.arange(2 * WINDOW)[None, :]
    rel = qi + WINDOW - ki
    in_band = (rel >= 0) & (rel < WINDOW)
    blk = jnp.arange(nb)[:, None, None]
    valid = in_band[None] & ((blk > 0) | (ki[None] >= WINDOW))
    s = jnp.where(valid[None, :, None, None], s, -jnp.inf)
    sink = jnp.broadcast_to(
        sinks.astype(jnp.float32).reshape(1, 1, N_KV_HEADS, GQA_GROUP, 1, 1),
        s.shape[:-1] + (1,))
    p = jax.nn.softmax(jnp.concatenate([s, sink], axis=-1), axis=-1)[..., :-1]
    o = jnp.einsum('bnhgqk,bnkhd->bnqhgd', p.astype(v.dtype), vb)
    return o.reshape(B, S, N_Q_HEADS * HEAD_DIM)


def conformer_conv(u, conv_w, conv_b, ln_g, ln_b):
    a, gate = jnp.split(u, 2, axis=-1)
    h = a * jax.nn.sigmoid(gate)
    h = lax.conv_general_dilated(
        h, conv_w.astype(h.dtype), window_strides=(1,),
        padding=[(CONV_WIDTH - 1, 0)],
        dimension_numbers=('NWC', 'WIO', 'NWC'),
        feature_group_count=CONV_CH) + conv_b
    h = layer_norm(h, ln_g, ln_b)
    return jax.nn.silu(h)


def setup_inputs(seed: int = 0) -> dict:
    key = jax.random.key(seed)
    ks = jax.random.split(key, 24)

    def nrm(k, shape, scale):
        return jax.random.normal(k, shape, jnp.float32) * scale

    def gain(k, shape):
        return 1.0 + 0.02 * jax.random.normal(k, shape, jnp.float32)

    offsets = jax.random.randint(ks[2], (BATCH, 1), 0, 2048, dtype=jnp.int32)
    positions = offsets + jnp.arange(SEQ, dtype=jnp.int32)[None, :]
    return {
        'x': nrm(ks[0], (BATCH, SEQ, D_MODEL), 1.0),
        'c': nrm(ks[1], (BATCH, D_MODEL), 1.0),
        'positions': positions,
        'w_ada': nrm(ks[3], (DEPTH, D_MODEL, N_MOD * D_MODEL), 0.5 * D_MODEL ** -0.5),
        'b_ada': nrm(ks[4], (DEPTH, N_MOD * D_MODEL), 0.01),
        'norm1_g': gain(ks[5], (DEPTH, D_MODEL)),
        'w_in': nrm(ks[6], (DEPTH, D_MODEL, IN_COLS), D_MODEL ** -0.5),
        'q_norm_g': gain(ks[7], (DEPTH, HEAD_DIM)),
        'k_norm_g': gain(ks[8], (DEPTH, HEAD_DIM)),
        'sinks': nrm(ks[9], (DEPTH, N_Q_HEADS), 1.0),
        'conv_w': nrm(ks[10], (DEPTH, CONV_WIDTH, 1, CONV_CH), CONV_WIDTH ** -0.5),
        'conv_b': nrm(ks[11], (DEPTH, CONV_CH), 0.02),
        'conv_ln_g': gain(ks[12], (DEPTH, CONV_CH)),
        'conv_ln_b': nrm(ks[13], (DEPTH, CONV_CH), 0.02),
        'attn_out_g': gain(ks[14], (DEPTH, ATTN_WIDTH)),
        'conv_out_g': gain(ks[15], (DEPTH, CONV_CH)),
        'w_out': nrm(ks[16], (DEPTH, D_MODEL, D_MODEL), D_MODEL ** -0.5),
        'norm2_g': gain(ks[17], (DEPTH, D_MODEL)),
        'w_ffn_gate': nrm(ks[18], (DEPTH, D_MODEL, D_FF), D_MODEL ** -0.5),
        'w_ffn_up': nrm(ks[19], (DEPTH, D_MODEL, D_FF), D_MODEL ** -0.5),
        'w_ffn_down': nrm(ks[20], (DEPTH, D_FF, D_MODEL), D_FF ** -0.5),
    }


def reference(x, c, positions, w_ada, b_ada, norm1_g, w_in, q_norm_g, k_norm_g,
              sinks, conv_w, conv_b, conv_ln_g, conv_ln_b, attn_out_g, conv_out_g,
              w_out, norm2_g, w_ffn_gate, w_ffn_up, w_ffn_down):
    B, S = x.shape[0], x.shape[1]
    cos, sin = rotary_tables(positions)
    c_act = jax.nn.silu(c)
    for l in range(DEPTH):
        mod = c_act @ w_ada[l] + b_ada[l]
        sh1, sc1, g1, sh2, sc2, g2 = [m[:, None, :] for m in jnp.split(mod, N_MOD, axis=-1)]

        h = rms_norm(x, norm1_g[l]) * (1.0 + sc1) + sh1
        proj = h @ w_in[l]
        q, k, v, u = jnp.split(proj, [Q_COLS, Q_COLS + KV_COLS, Q_COLS + 2 * KV_COLS], axis=-1)
        q = q.reshape(B, S, N_Q_HEADS, HEAD_DIM)
        k = k.reshape(B, S, N_KV_HEADS, HEAD_DIM)
        v = v.reshape(B, S, N_KV_HEADS, HEAD_DIM)
        q = apply_partial_rotary(rms_norm(q, q_norm_g[l]), cos, sin)
        k = apply_partial_rotary(rms_norm(k, k_norm_g[l]), cos, sin)
        attn = sliding_window_sink_attention(q, k, v, sinks[l])
        conv = conformer_conv(u, conv_w[l], conv_b[l], conv_ln_g[l], conv_ln_b[l])
        mixed = jnp.concatenate([rms_norm(attn, attn_out_g[l]),
                                 rms_norm(conv, conv_out_g[l])], axis=-1)
        x = x + g1 * (mixed @ w_out[l])

        h = rms_norm(x, norm2_g[l]) * (1.0 + sc2) + sh2
        ffn = (jax.nn.silu(h @ w_ffn_gate[l]) * (h @ w_ffn_up[l])) @ w_ffn_down[l]
        x = x + g2 * ffn
    return x
```

```python
import functools

import jax
import jax.numpy as jnp
from jax import lax
from jax.experimental import pallas as pl
from jax.experimental.pallas import tpu as pltpu

HEAD_DIM = 64
GQA_GROUP = 8
WINDOW = 128
ROT_DIM = HEAD_DIM // 4
ROPE_THETA = 500000.0
CONV_WIDTH = 31
N_MOD = 6

LANES = 128
HALO = 32
VMEM_LIMIT = 56 * 1024 * 1024

BF16 = jnp.bfloat16
F32 = jnp.float32
NEG_BIG = -1e30


def _params(*sem):
    return pltpu.CompilerParams(dimension_semantics=sem, vmem_limit_bytes=VMEM_LIMIT)


def _ada_kernel(c_ref, w_ref, b_ref, o_ref):
    c = c_ref[...]
    ca = (c * jax.nn.sigmoid(c)).astype(BF16)
    acc = jnp.dot(ca, w_ref[...].astype(BF16), preferred_element_type=F32)
    o_ref[...] = acc + b_ref[...]


def _ada_mod(c8, w_ada, b_ada, tn=512):
    depth, d, n = w_ada.shape
    return pl.pallas_call(
        _ada_kernel,
        out_shape=jax.ShapeDtypeStruct((depth, 8, n), F32),
        grid=(depth, n // tn),
        in_specs=[
            pl.BlockSpec((8, d), lambda l, j: (0, 0)),
            pl.BlockSpec((None, d, tn), lambda l, j: (l, 0, j)),
            pl.BlockSpec((None, 1, tn), lambda l, j: (l, 0, j)),
        ],
        out_specs=pl.BlockSpec((None, 8, tn), lambda l, j: (l, 0, j)),
        compiler_params=_params("parallel", "arbitrary"),
        name="ada_mod",
    )(c8, w_ada, b_ada.reshape(depth, 1, n))


def _rope_kernel(pos_ref, invf_ref, c_ref, s1_ref, s2_ref):
    ang = pos_ref[...].astype(F32) * invf_ref[...]
    cosv = jnp.cos(ang)
    sinv = jnp.sin(ang)
    d = lax.broadcasted_iota(jnp.int32, ang.shape, 1) % HEAD_DIM
    half = ROT_DIM // 2
    c_ref[...] = jnp.where(d < ROT_DIM, cosv, 1.0)
    s1_ref[...] = jnp.where(d < half, -sinv, 0.0)
    s2_ref[...] = jnp.where((d >= half) & (d < ROT_DIM), sinv, 0.0)


def _rope_tables(positions, rows=1024):
    m = positions.size
    inv_freq = ROPE_THETA ** (-jnp.arange(0, ROT_DIM, 2, dtype=F32) / ROT_DIM)
    lane_freq = jnp.tile(inv_freq, LANES // (ROT_DIM // 2)).reshape(1, LANES)
    shp = jax.ShapeDtypeStruct((m, LANES), F32)
    spec = pl.BlockSpec((rows, LANES), lambda i: (i, 0))
    return pl.pallas_call(
        _rope_kernel,
        out_shape=(shp, shp, shp),
        grid=(m // rows,),
        in_specs=[pl.BlockSpec((rows, 1), lambda i: (i, 0)),
                  pl.BlockSpec((1, LANES), lambda i: (0, 0))],
        out_specs=(spec, spec, spec),
        compiler_params=_params("parallel"),
        name="rope_tables",
    )(positions.reshape(m, 1), lane_freq)


def _norm_mod_kernel(x_ref, g_ref, sc_ref, sh_ref, o_ref):
    x = x_ref[...]
    ms = jnp.mean(x * x, axis=-1, keepdims=True)
    y = x * lax.rsqrt(ms + 1e-6) * g_ref[...]
    o_ref[...] = (y * (1.0 + sc_ref[...]) + sh_ref[...]).astype(BF16)


def _norm_mod(x2, gain, mod5, l, sc_idx, sh_idx, seq, rows=256):
    m, d = x2.shape
    per_b = seq // rows
    return pl.pallas_call(
        _norm_mod_kernel,
        out_shape=jax.ShapeDtypeStruct((m, d), BF16),
        grid=(m // rows,),
        in_specs=[
            pl.BlockSpec((rows, d), lambda i: (i, 0)),
            pl.BlockSpec((None, 1, d), lambda i: (l, 0, 0)),
            pl.BlockSpec((None, None, None, 1, d), lambda i: (l, i // per_b, sc_idx, 0, 0)),
            pl.BlockSpec((None, None, None, 1, d), lambda i: (l, i // per_b, sh_idx, 0, 0)),
        ],
        out_specs=pl.BlockSpec((rows, d), lambda i: (i, 0)),
        compiler_params=_params("parallel"),
        name="norm_mod",
    )(x2, gain, mod5, mod5)


def _head_norm_rope(y, g, c, s1, s2, scale):
    sq = y * y
    lo = lax.broadcasted_iota(jnp.int32, y.shape, 1) < HEAD_DIM
    s_lo = jnp.sum(jnp.where(lo, sq, 0.0), axis=-1, keepdims=True)
    s_hi = jnp.sum(jnp.where(lo, 0.0, sq), axis=-1, keepdims=True)
    ms = jnp.where(lo, s_lo, s_hi) * (1.0 / HEAD_DIM)
    yn = y * lax.rsqrt(ms + 1e-6) * g
    half = ROT_DIM // 2
    out = yn * c + pltpu.roll(yn, LANES - half, 1) * s1 + pltpu.roll(yn, half, 1) * s2
    return out * scale if scale != 1.0 else out


def _qkv_kernel(h_ref, w_ref, qg_ref, kg_ref, c_ref, s1_ref, s2_ref, o_ref, *,
                n_q_tiles, kv_cols, tn):
    j = pl.program_id(1)
    acc = jnp.dot(h_ref[...], w_ref[...].astype(BF16), preferred_element_type=F32)
    c, s1, s2 = c_ref[...], s1_ref[...], s2_ref[...]

    @pl.when(j < n_q_tiles)
    def _():
        g = qg_ref[...]
        for cc in range(tn // LANES):
            sl = slice(cc * LANES, (cc + 1) * LANES)
            o_ref[:, sl] = _head_norm_rope(acc[:, sl], g, c, s1, s2,
                                           HEAD_DIM ** -0.5).astype(BF16)

    @pl.when(j == n_q_tiles)
    def _():
        g = kg_ref[...]
        for cc in range(tn // LANES):
            sl = slice(cc * LANES, (cc + 1) * LANES)
            if cc * LANES < kv_cols:
                o_ref[:, sl] = _head_norm_rope(acc[:, sl], g, c, s1, s2, 1.0).astype(BF16)
            else:
                o_ref[:, sl] = acc[:, sl].astype(BF16)


def _qkv_proj(h, w_in, l, qg, kg, tabs, q_cols, kv_cols, tm=1024, tn=512):
    m, d = h.shape
    assert q_cols % tn == 0 and 2 * kv_cols == tn
    n_q_tiles = q_cols // tn
    n_out = q_cols + 2 * kv_cols
    tab_spec = pl.BlockSpec((tm, LANES), lambda i, j: (i, 0))
    gain_spec = pl.BlockSpec((None, 1, LANES), lambda i, j: (l, 0, 0))
    return pl.pallas_call(
        functools.partial(_qkv_kernel, n_q_tiles=n_q_tiles, kv_cols=kv_cols, tn=tn),
        out_shape=jax.ShapeDtypeStruct((m, n_out), BF16),
        grid=(m // tm, n_out // tn),
        in_specs=[
            pl.BlockSpec((tm, d), lambda i, j: (i, 0)),
            pl.BlockSpec((None, d, tn), lambda i, j: (l, 0, j)),
            gain_spec, gain_spec, tab_spec, tab_spec, tab_spec,
        ],
        out_specs=pl.BlockSpec((tm, tn), lambda i, j: (i, j)),
        compiler_params=_params("parallel", "arbitrary"),
        name="qkv_proj",
    )(h, w_in, qg, kg, *tabs)


def _pair_kernel(h_ref, wa_ref, wb_ref, o_ref, *, swiglu):
    h = h_ref[...]
    a = jnp.dot(h, wa_ref[...].astype(BF16), preferred_element_type=F32)
    b = jnp.dot(h, wb_ref[...].astype(BF16), preferred_element_type=F32)
    if swiglu:
        o_ref[...] = ((a * jax.nn.sigmoid(a)) * b).astype(o_ref.dtype)
    else:
        o_ref[...] = (a * jax.nn.sigmoid(b)).astype(o_ref.dtype)


def _pair_proj(h, wa, wb, l, a_off, b_off, n, out_dtype, swiglu, name, tm=1024, tn=256):
    m, d = h.shape
    assert a_off % tn == 0 and b_off % tn == 0 and n % tn == 0
    ja, jb = a_off // tn, b_off // tn
    return pl.pallas_call(
        functools.partial(_pair_kernel, swiglu=swiglu),
        out_shape=jax.ShapeDtypeStruct((m, n), out_dtype),
        grid=(m // tm, n // tn),
        in_specs=[
            pl.BlockSpec((tm, d), lambda i, j: (i, 0)),
            pl.BlockSpec((None, d, tn), lambda i, j: (l, 0, ja + j)),
            pl.BlockSpec((None, d, tn), lambda i, j: (l, 0, jb + j)),
        ],
        out_specs=pl.BlockSpec((tm, tn), lambda i, j: (i, j)),
        compiler_params=_params("parallel", "arbitrary"),
        name=name,
    )(h, wa, wb)


def _attn_kernel(sink_ref, q_ref, kvp_ref, kvc_ref, g_ref, o_ref, acc_ref, *,
                 n_kv, n_pairs_per_kv):
    n = pl.program_id(1)
    w = WINDOW
    qi = lax.broadcasted_iota(jnp.int32, (w, 2 * w), 0)
    ki = lax.broadcasted_iota(jnp.int32, (w, 2 * w), 1)
    rel = qi + w - ki
    first_key = jnp.where(n > 0, 0, w)
    valid = (rel >= 0) & (rel < w) & (ki >= first_key)
    bias = jnp.where(valid, 0.0, NEG_BIG).astype(F32)
    lo_o = lax.broadcasted_iota(jnp.int32, (w, LANES), 1) < HEAD_DIM
    lo_k = lax.broadcasted_iota(jnp.int32, (2 * w, LANES), 1) < HEAD_DIM
    kv_cols = n_kv * HEAD_DIM
    dn_t = (((1,), (1,)), ((), ()))

    def softmax_parts(s, sink):
        mx = jnp.maximum(jnp.max(s, axis=-1, keepdims=True), sink)
        p = jnp.exp(s - mx)
        den = jnp.sum(p, axis=-1, keepdims=True) + jnp.exp(sink - mx)
        return p.astype(BF16), den

    for kb in range(n_kv // 2):
        ksl = slice(kb * LANES, (kb + 1) * LANES)
        vsl = slice(kv_cols + kb * LANES, kv_cols + (kb + 1) * LANES)
        k2 = jnp.concatenate([kvp_ref[:, ksl], kvc_ref[:, ksl]], axis=0).astype(F32)
        v2 = jnp.concatenate([kvp_ref[:, vsl], kvc_ref[:, vsl]], axis=0).astype(F32)
        k2r = pltpu.roll(k2, HEAD_DIM, 1)
        v2r = pltpu.roll(v2, HEAD_DIM, 1)
        for sub in range(2):
            kvh = 2 * kb + sub
            k_at_lo, k_at_hi = (k2, k2r) if sub == 0 else (k2r, k2)
            v_at_lo, v_at_hi = (v2, v2r) if sub == 0 else (v2r, v2)
            k_e = jnp.where(lo_k, k_at_lo, 0.0).astype(BF16)
            k_o = jnp.where(lo_k, 0.0, k_at_hi).astype(BF16)
            v_e = jnp.where(lo_k, v_at_lo, 0.0).astype(BF16)
            v_o = jnp.where(lo_k, 0.0, v_at_hi).astype(BF16)
            for p in range(n_pairs_per_kv):
                hp = kvh * n_pairs_per_kv + p
                sl = slice(hp * LANES, (hp + 1) * LANES)
                qp = q_ref[:, sl]
                s_e = lax.dot_general(qp, k_e, dn_t, preferred_element_type=F32) + bias
                s_o = lax.dot_general(qp, k_o, dn_t, preferred_element_type=F32) + bias
                p_e, d_e = softmax_parts(s_e, sink_ref[0, 2 * hp])
                p_o, d_o = softmax_parts(s_o, sink_ref[0, 2 * hp + 1])
                o = (jnp.dot(p_e, v_e, preferred_element_type=F32)
                     + jnp.dot(p_o, v_o, preferred_element_type=F32))
                acc_ref[:, sl] = o / jnp.where(lo_o, d_e, d_o)

    a = acc_ref[...]
    ms = jnp.mean(a * a, axis=-1, keepdims=True)
    o_ref[...] = (a * lax.rsqrt(ms + 1e-6) * g_ref[...]).astype(BF16)


def _attention(qkv, sinks_l, gain, l, seq, q_cols, kv_cols):
    m = qkv.shape[0]
    nb = seq // WINDOW
    n_kv = kv_cols // HEAD_DIM
    n_pairs_per_kv = GQA_GROUP // 2
    assert 2 * kv_cols == 4 * LANES and q_cols % (2 * kv_cols) == 0
    kv_blk = q_cols // (2 * kv_cols)
    return pl.pallas_call(
        functools.partial(_attn_kernel, n_kv=n_kv, n_pairs_per_kv=n_pairs_per_kv),
        out_shape=jax.ShapeDtypeStruct((m, q_cols), BF16),
        grid=(m // seq, nb),
        in_specs=[
            pl.BlockSpec(memory_space=pltpu.SMEM),
            pl.BlockSpec((WINDOW, q_cols), lambda b, n: (b * nb + n, 0)),
            pl.BlockSpec((WINDOW, 2 * kv_cols),
                         lambda b, n: (b * nb + jnp.maximum(n - 1, 0), kv_blk)),
            pl.BlockSpec((WINDOW, 2 * kv_cols), lambda b, n: (b * nb + n, kv_blk)),
            pl.BlockSpec((None, 1, q_cols), lambda b, n: (l, 0, 0)),
        ],
        out_specs=pl.BlockSpec((WINDOW, q_cols), lambda b, n: (b * nb + n, 0)),
        scratch_shapes=[pltpu.VMEM((WINDOW, q_cols), F32)],
        compiler_params=_params("parallel", "arbitrary"),
        name="swa_attention",
    )(sinks_l, qkv, qkv, qkv, gain)


def _conv_kernel(xc_ref, xh_ref, w_ref, cb_ref, lg_ref, lb_ref, og_ref, o_ref,
                 xp_ref, y_ref, *, tt, rows):
    t = pl.program_id(1)
    ch = xc_ref.shape[1]
    xp_ref[0:HALO, :] = jnp.where(t > 0, xh_ref[...], 0.0)
    xp_ref[HALO:HALO + tt, :] = xc_ref[...]
    first = HALO - (CONV_WIDTH - 1)

    def lane_chunk(ci, carry):
        c0 = pl.multiple_of(ci * LANES, LANES)
        wts = w_ref[:, pl.ds(c0, LANES)]
        cb = cb_ref[:, pl.ds(c0, LANES)]
        for r0 in range(0, tt, rows):
            acc = jnp.zeros((rows, LANES), F32)
            for k in range(CONV_WIDTH):
                acc = acc + xp_ref[first + k + r0:first + k + r0 + rows, pl.ds(c0, LANES)] * wts[k:k + 1, :]
            y_ref[r0:r0 + rows, pl.ds(c0, LANES)] = acc + cb
        return carry

    lax.fori_loop(0, ch // LANES, lane_chunk, 0)

    y = y_ref[...]
    mu = jnp.mean(y, axis=-1, keepdims=True)
    dlt = y - mu
    var = jnp.mean(dlt * dlt, axis=-1, keepdims=True)
    z = dlt * lax.rsqrt(var + 1e-5) * lg_ref[...] + lb_ref[...]
    s = z * jax.nn.sigmoid(z)
    ms = jnp.mean(s * s, axis=-1, keepdims=True)
    o_ref[...] = (s * lax.rsqrt(ms + 1e-6) * og_ref[...]).astype(BF16)


def _conv_module(glu, conv_w, conv_b, ln_g, ln_b, out_g, l, seq, tt=256, rows=64):
    m, ch = glu.shape
    nt = seq // tt
    hb = tt // HALO
    vec = pl.BlockSpec((None, 1, ch), lambda b, t: (l, 0, 0))
    return pl.pallas_call(
        functools.partial(_conv_kernel, tt=tt, rows=rows),
        out_shape=jax.ShapeDtypeStruct((m, ch), BF16),
        grid=(m // seq, nt),
        in_specs=[
            pl.BlockSpec((tt, ch), lambda b, t: (b * nt + t, 0)),
            pl.BlockSpec((HALO, ch), lambda b, t: (jnp.maximum((b * nt + t) * hb - 1, 0), 0)),
            pl.BlockSpec((None, CONV_WIDTH, ch), lambda b, t: (l, 0, 0)),
            vec, vec, vec, vec,
        ],
        out_specs=pl.BlockSpec((tt, ch), lambda b, t: (b * nt + t, 0)),
        scratch_shapes=[pltpu.VMEM((HALO + tt, ch), F32), pltpu.VMEM((tt, ch), F32)],
        compiler_params=_params("parallel", "arbitrary"),
        name="conv_module",
    )(glu, glu, conv_w, conv_b, ln_g, ln_b, out_g)


def _out_proj_kernel(a_ref, c_ref, w_ref, g_ref, x_ref, o_ref):
    ka = a_ref.shape[1]
    w = w_ref[...].astype(BF16)
    acc = (jnp.dot(a_ref[...], w[:ka], preferred_element_type=F32)
           + jnp.dot(c_ref[...], w[ka:], preferred_element_type=F32))
    o_ref[...] = x_ref[...] + g_ref[...] * acc


def _out_proj(attn, conv, w_out, mod5, l, x2, seq, tm=1024, tn=512):
    m, ka = attn.shape
    kc = conv.shape[1]
    d = x2.shape[1]
    per_b = seq // tm
    return pl.pallas_call(
        _out_proj_kernel,
        out_shape=jax.ShapeDtypeStruct((m, d), F32),
        grid=(m // tm, d // tn),
        in_specs=[
            pl.BlockSpec((tm, ka), lambda i, j: (i, 0)),
            pl.BlockSpec((tm, kc), lambda i, j: (i, 0)),
            pl.BlockSpec((None, ka + kc, tn), lambda i, j: (l, 0, j)),
            pl.BlockSpec((None, None, None, 1, tn), lambda i, j: (l, i // per_b, 2, 0, j)),
            pl.BlockSpec((tm, tn), lambda i, j: (i, j)),
        ],
        out_specs=pl.BlockSpec((tm, tn), lambda i, j: (i, j)),
        compiler_params=_params("parallel", "arbitrary"),
        name="out_proj",
    )(attn, conv, w_out, mod5, x2)


def _down_kernel(a_ref, w_ref, g_ref, x_ref, o_ref, *, x_rows, x_steps):
    k = pl.program_id(1)
    @pl.when(k == 0)
    def _():
        o_ref[...] = jnp.zeros_like(o_ref)

    o_ref[...] += jnp.dot(a_ref[...], w_ref[...].astype(BF16),
                          preferred_element_type=F32) * g_ref[...]

    @pl.when(k < x_steps)
    def _():
        r0 = pl.multiple_of(k * x_rows, x_rows)
        o_ref[pl.ds(r0, x_rows), :] += x_ref[...]


def _down_proj(act, w_down, mod5, l, x2, seq, tm=1024, tk=256):
    m, f = act.shape
    d = x2.shape[1]
    n_k = f // tk
    x_steps = 32
    x_rows = tm // x_steps
    assert f % tk == 0 and n_k >= x_steps and x_rows % 8 == 0
    per_b = seq // tm
    return pl.pallas_call(
        functools.partial(_down_kernel, x_rows=x_rows, x_steps=x_steps),
        out_shape=jax.ShapeDtypeStruct((m, d), F32),
        grid=(m // tm, n_k),
        in_specs=[
            pl.BlockSpec((tm, tk), lambda i, k: (i, k)),
            pl.BlockSpec((None, tk, d), lambda i, k: (l, k, 0)),
            pl.BlockSpec((None, None, None, 1, d), lambda i, k: (l, i // per_b, 5, 0, 0)),
            pl.BlockSpec((x_rows, d), lambda i, k: (i * x_steps + jnp.minimum(k, x_steps - 1), 0)),
        ],
        out_specs=pl.BlockSpec((tm, d), lambda i, k: (i, 0)),
        compiler_params=_params("parallel", "arbitrary"),
        name="ffn_down",
    )(act, w_down, mod5, x2)


def kernel(x, c, positions, w_ada, b_ada, norm1_g, w_in, q_norm_g, k_norm_g, sinks,
           conv_w, conv_b, conv_ln_g, conv_ln_b, attn_out_g, conv_out_g, w_out,
           norm2_g, w_ffn_gate, w_ffn_up, w_ffn_down):
    bsz, seq, d = x.shape
    depth = w_ada.shape[0]
    q_cols = attn_out_g.shape[1]
    conv_ch = conv_out_g.shape[1]
    kv_cols = (w_in.shape[2] - q_cols - 2 * conv_ch) // 2
    d_ff = w_ffn_gate.shape[2]
    m = bsz * seq

    c8 = jnp.zeros((8, d), F32).at[:bsz].set(c)
    mod = _ada_mod(c8, w_ada, b_ada)
    mod5 = mod.reshape(depth, 8, N_MOD, 1, d)
    tabs = _rope_tables(positions)

    def vec3(p):
        return p.reshape(depth, 1, p.shape[-1])

    qg = vec3(jnp.tile(q_norm_g, (1, LANES // HEAD_DIM)))
    kg = vec3(jnp.tile(k_norm_g, (1, LANES // HEAD_DIM)))
    n1, n2 = vec3(norm1_g), vec3(norm2_g)
    cw = conv_w.reshape(depth, CONV_WIDTH, conv_ch)
    cb, lg, lb = vec3(conv_b), vec3(conv_ln_g), vec3(conv_ln_b)
    ag, og = vec3(attn_out_g), vec3(conv_out_g)

    x2 = x.reshape(m, d)
    u_off = q_cols + 2 * kv_cols
    for l in range(depth):
        h = _norm_mod(x2, n1, mod5, l, 1, 0, seq)
        qkv = _qkv_proj(h, w_in, l, qg, kg, tabs, q_cols, kv_cols)
        glu = _pair_proj(h, w_in, w_in, l, u_off, u_off + conv_ch, conv_ch, F32,
                         False, "glu_proj")
        attn = _attention(qkv, sinks[l].reshape(1, -1), ag, l, seq, q_cols, kv_cols)
        conv = _conv_module(glu, cw, cb, lg, lb, og, l, seq)
        x2 = _out_proj(attn, conv, w_out, mod5, l, x2, seq)
        h = _norm_mod(x2, n2, mod5, l, 4, 3, seq)
        act = _pair_proj(h, w_ffn_gate, w_ffn_up, l, 0, 0, d_ff, BF16, True, "ffn_gate_up")
        x2 = _down_proj(act, w_ffn_down, mod5, l, x2, seq)
    return x2.reshape(bsz, seq, d)
```

```python
import functools

import jax
import jax.numpy as jnp
from jax import lax
from jax.experimental import pallas as pl
from jax.experimental.pallas import tpu as pltpu

HEAD_DIM = 64
GQA_GROUP = 8
WINDOW = 128
ROT_DIM = HEAD_DIM // 4
ROPE_THETA = 500000.0
CONV_WIDTH = 31
N_MOD = 6

LANES = 128
SUBLANES = 8
HALO = 32
TM_WIDE = 2048
VMEM_LIMIT = 56 * 1024 * 1024
DOWN_VMEM_LIMIT = 60 * 1024 * 1024

BF16 = jnp.bfloat16
F32 = jnp.float32
NEG_BIG = -1e30


def _params(*sem):
    return pltpu.CompilerParams(dimension_semantics=sem, vmem_limit_bytes=VMEM_LIMIT)


def _ada_kernel(c_ref, w_ref, b_ref, o_ref):
    c = c_ref[...]
    ca = (c * jax.nn.sigmoid(c)).astype(BF16)
    acc = jnp.dot(ca, w_ref[...].astype(BF16), preferred_element_type=F32)
    o_ref[...] = acc + b_ref[...]


def _ada_mod(c8, w_ada, b_ada, tn=512):
    depth, d, n = w_ada.shape
    return pl.pallas_call(
        _ada_kernel,
        out_shape=jax.ShapeDtypeStruct((depth, 8, n), F32),
        grid=(depth, n // tn),
        in_specs=[
            pl.BlockSpec((8, d), lambda l, j: (0, 0)),
            pl.BlockSpec((None, d, tn), lambda l, j: (l, 0, j)),
            pl.BlockSpec((None, 1, tn), lambda l, j: (l, 0, j)),
        ],
        out_specs=pl.BlockSpec((None, 8, tn), lambda l, j: (l, 0, j)),
        compiler_params=_params("parallel", "arbitrary"),
        name="ada_mod",
    )(c8, w_ada, b_ada.reshape(depth, 1, n))


def _rope_kernel(pos_ref, invf_ref, c_ref, s1_ref, s2_ref):
    ang = pos_ref[...].astype(F32) * invf_ref[...]
    cosv = jnp.cos(ang)
    sinv = jnp.sin(ang)
    d = lax.broadcasted_iota(jnp.int32, ang.shape, 1) % HEAD_DIM
    half = ROT_DIM // 2
    c_ref[...] = jnp.where(d < ROT_DIM, cosv, 1.0)
    s1_ref[...] = jnp.where(d < half, -sinv, 0.0)
    s2_ref[...] = jnp.where((d >= half) & (d < ROT_DIM), sinv, 0.0)


def _rope_tables(positions, rows=1024):
    m = positions.size
    inv_freq = ROPE_THETA ** (-jnp.arange(0, ROT_DIM, 2, dtype=F32) / ROT_DIM)
    lane_freq = jnp.tile(inv_freq, LANES // (ROT_DIM // 2)).reshape(1, LANES)
    shp = jax.ShapeDtypeStruct((m, LANES), F32)
    spec = pl.BlockSpec((rows, LANES), lambda i: (i, 0))
    return pl.pallas_call(
        _rope_kernel,
        out_shape=(shp, shp, shp),
        grid=(m // rows,),
        in_specs=[pl.BlockSpec((rows, 1), lambda i: (i, 0)),
                  pl.BlockSpec((1, LANES), lambda i: (0, 0))],
        out_specs=(spec, spec, spec),
        compiler_params=_params("parallel"),
        name="rope_tables",
    )(positions.reshape(m, 1), lane_freq)


def _norm_mod_kernel(x_ref, g_ref, sc_ref, sh_ref, o_ref):
    x = x_ref[...]
    ms = jnp.mean(x * x, axis=-1, keepdims=True)
    y = x * lax.rsqrt(ms + 1e-6) * g_ref[...]
    o_ref[...] = (y * (1.0 + sc_ref[...]) + sh_ref[...]).astype(BF16)


def _norm_mod(x2, gain, mod5, l, sc_idx, sh_idx, seq, rows=256):
    m, d = x2.shape
    per_b = seq // rows
    return pl.pallas_call(
        _norm_mod_kernel,
        out_shape=jax.ShapeDtypeStruct((m, d), BF16),
        grid=(m // rows,),
        in_specs=[
            pl.BlockSpec((rows, d), lambda i: (i, 0)),
            pl.BlockSpec((None, 1, d), lambda i: (l, 0, 0)),
            pl.BlockSpec((None, None, None, 1, d), lambda i: (l, i // per_b, sc_idx, 0, 0)),
            pl.BlockSpec((None, None, None, 1, d), lambda i: (l, i // per_b, sh_idx, 0, 0)),
        ],
        out_specs=pl.BlockSpec((rows, d), lambda i: (i, 0)),
        compiler_params=_params("parallel"),
        name="norm_mod",
    )(x2, gain, mod5, mod5)


def _head_norm_rope(y, g, c, s1, s2, scale):
    sq = y * y
    lo = lax.broadcasted_iota(jnp.int32, y.shape, 1) < HEAD_DIM
    s_lo = jnp.sum(jnp.where(lo, sq, 0.0), axis=-1, keepdims=True)
    s_hi = jnp.sum(jnp.where(lo, 0.0, sq), axis=-1, keepdims=True)
    ms = jnp.where(lo, s_lo, s_hi) * (1.0 / HEAD_DIM)
    yn = y * lax.rsqrt(ms + 1e-6) * g
    half = ROT_DIM // 2
    out = yn * c + pltpu.roll(yn, LANES - half, 1) * s1 + pltpu.roll(yn, half, 1) * s2
    return out * scale if scale != 1.0 else out


def _head_proj_kernel(h_ref, w_ref, g_ref, c_ref, s1_ref, s2_ref, o_ref, *,
                      normed_cols, scale, row_chunk):
    w = w_ref[...].astype(BF16)
    g = g_ref[...]
    for r0 in range(0, h_ref.shape[0], row_chunk):
        rows = slice(r0, r0 + row_chunk)
        acc = jnp.dot(h_ref[rows, :], w, preferred_element_type=F32)
        c, s1, s2 = c_ref[rows, :], s1_ref[rows, :], s2_ref[rows, :]
        for cc in range(acc.shape[1] // LANES):
            sl = slice(cc * LANES, (cc + 1) * LANES)
            if cc * LANES < normed_cols:
                o_ref[rows, sl] = _head_norm_rope(acc[:, sl], g, c, s1, s2, scale).astype(BF16)
            else:
                o_ref[rows, sl] = acc[:, sl].astype(BF16)


def _head_proj(h, w_in, l, gain, tabs, col_off, n_out, normed_cols, scale, name,
               tm, tn=512):
    m, d = h.shape
    normed_cols = tn if normed_cols is None else normed_cols
    assert col_off % tn == 0 and n_out % tn == 0 and normed_cols % LANES == 0
    j0 = col_off // tn
    tab_spec = pl.BlockSpec((tm, LANES), lambda i, j: (i, 0))
    return pl.pallas_call(
        functools.partial(_head_proj_kernel, normed_cols=normed_cols, scale=scale,
                          row_chunk=256),
        out_shape=jax.ShapeDtypeStruct((m, n_out), BF16),
        grid=(m // tm, n_out // tn),
        in_specs=[
            pl.BlockSpec((tm, d), lambda i, j: (i, 0), pipeline_mode=pl.Buffered(1)),
            pl.BlockSpec((None, d, tn), lambda i, j: (l, 0, j0 + j)),
            pl.BlockSpec((None, 1, LANES), lambda i, j: (l, 0, 0)),
            tab_spec, tab_spec, tab_spec,
        ],
        out_specs=pl.BlockSpec((tm, tn), lambda i, j: (i, j)),
        compiler_params=_params("parallel", "arbitrary"),
        name=name,
    )(h, w_in, gain, *tabs)


def _pair_kernel(h_ref, wa_ref, wb_ref, o_ref, *, swiglu):
    h = h_ref[...]
    a = jnp.dot(h, wa_ref[...].astype(BF16), preferred_element_type=F32)
    b = jnp.dot(h, wb_ref[...].astype(BF16), preferred_element_type=F32)
    if swiglu:
        o_ref[...] = ((a * jax.nn.sigmoid(a)) * b).astype(o_ref.dtype)
    else:
        o_ref[...] = (a * jax.nn.sigmoid(b)).astype(o_ref.dtype)


def _pair_proj(h, wa, wb, l, a_off, b_off, n, out_dtype, swiglu, name, tm, tn=256):
    m, d = h.shape
    assert a_off % tn == 0 and b_off % tn == 0 and n % tn == 0
    ja, jb = a_off // tn, b_off // tn
    return pl.pallas_call(
        functools.partial(_pair_kernel, swiglu=swiglu),
        out_shape=jax.ShapeDtypeStruct((m, n), out_dtype),
        grid=(m // tm, n // tn),
        in_specs=[
            pl.BlockSpec((tm, d), lambda i, j: (i, 0), pipeline_mode=pl.Buffered(1)),
            pl.BlockSpec((None, d, tn), lambda i, j: (l, 0, ja + j)),
            pl.BlockSpec((None, d, tn), lambda i, j: (l, 0, jb + j)),
        ],
        out_specs=pl.BlockSpec((tm, tn), lambda i, j: (i, j)),
        compiler_params=_params("parallel", "arbitrary"),
        name=name,
    )(h, wa, wb)


def _attn_kernel(sink_ref, q_ref, kvp_ref, kvc_ref, g_ref, o_ref, acc_ref, *,
                 n_kv, n_pairs_per_kv):
    n = pl.program_id(1)
    w = WINDOW
    qi = lax.broadcasted_iota(jnp.int32, (w, 2 * w), 0)
    ki = lax.broadcasted_iota(jnp.int32, (w, 2 * w), 1)
    rel = qi + w - ki
    first_key = jnp.where(n > 0, 0, w)
    valid = (rel >= 0) & (rel < w) & (ki >= first_key)
    bias = jnp.where(valid, 0.0, NEG_BIG).astype(F32)
    lo_o = lax.broadcasted_iota(jnp.int32, (w, LANES), 1) < HEAD_DIM
    lo_k = lax.broadcasted_iota(jnp.int32, (2 * w, LANES), 1) < HEAD_DIM
    kv_cols = n_kv * HEAD_DIM
    dn_t = (((1,), (1,)), ((), ()))

    def softmax_parts(s, sink):
        mx = jnp.maximum(jnp.max(s, axis=-1, keepdims=True), sink)
        p = jnp.exp(s - mx)
        den = jnp.sum(p, axis=-1, keepdims=True) + jnp.exp(sink - mx)
        return p.astype(BF16), den

    for kb in range(n_kv // 2):
        ksl = slice(kb * LANES, (kb + 1) * LANES)
        vsl = slice(kv_cols + kb * LANES, kv_cols + (kb + 1) * LANES)
        k2 = jnp.concatenate([kvp_ref[:, ksl], kvc_ref[:, ksl]], axis=0).astype(F32)
        v2 = jnp.concatenate([kvp_ref[:, vsl], kvc_ref[:, vsl]], axis=0).astype(F32)
        k2r = pltpu.roll(k2, HEAD_DIM, 1)
        v2r = pltpu.roll(v2, HEAD_DIM, 1)
        for sub in range(2):
            kvh = 2 * kb + sub
            k_at_lo, k_at_hi = (k2, k2r) if sub == 0 else (k2r, k2)
            v_at_lo, v_at_hi = (v2, v2r) if sub == 0 else (v2r, v2)
            k_e = jnp.where(lo_k, k_at_lo, 0.0).astype(BF16)
            k_o = jnp.where(lo_k, 0.0, k_at_hi).astype(BF16)
            v_e = jnp.where(lo_k, v_at_lo, 0.0).astype(BF16)
            v_o = jnp.where(lo_k, 0.0, v_at_hi).astype(BF16)
            for p in range(n_pairs_per_kv):
                hp = kvh * n_pairs_per_kv + p
                sl = slice(hp * LANES, (hp + 1) * LANES)
                qp = q_ref[:, sl]
                s_e = lax.dot_general(qp, k_e, dn_t, preferred_element_type=F32) + bias
                s_o = lax.dot_general(qp, k_o, dn_t, preferred_element_type=F32) + bias
                p_e, d_e = softmax_parts(s_e, sink_ref[0, 2 * hp])
                p_o, d_o = softmax_parts(s_o, sink_ref[0, 2 * hp + 1])
                o = (jnp.dot(p_e, v_e, preferred_element_type=F32)
                     + jnp.dot(p_o, v_o, preferred_element_type=F32))
                acc_ref[:, sl] = o / jnp.where(lo_o, d_e, d_o)

    a = acc_ref[...]
    ms = jnp.mean(a * a, axis=-1, keepdims=True)
    o_ref[...] = (a * lax.rsqrt(ms + 1e-6) * g_ref[...]).astype(BF16)


def _attention(q, kv, sinks_l, gain, l, seq):
    m, q_cols = q.shape
    kv_cols = kv.shape[1] // 2
    nb = seq // WINDOW
    n_kv = kv_cols // HEAD_DIM
    n_pairs_per_kv = GQA_GROUP // 2
    assert kv_cols % LANES == 0 and q_cols == n_kv * GQA_GROUP * HEAD_DIM
    return pl.pallas_call(
        functools.partial(_attn_kernel, n_kv=n_kv, n_pairs_per_kv=n_pairs_per_kv),
        out_shape=jax.ShapeDtypeStruct((m, q_cols), BF16),
        grid=(m // seq, nb),
        in_specs=[
            pl.BlockSpec(memory_space=pltpu.SMEM),
            pl.BlockSpec((WINDOW, q_cols), lambda b, n: (b * nb + n, 0)),
            pl.BlockSpec((WINDOW, 2 * kv_cols),
                         lambda b, n: (b * nb + jnp.maximum(n - 1, 0), 0)),
            pl.BlockSpec((WINDOW, 2 * kv_cols), lambda b, n: (b * nb + n, 0)),
            pl.BlockSpec((None, 1, q_cols), lambda b, n: (l, 0, 0)),
        ],
        out_specs=pl.BlockSpec((WINDOW, q_cols), lambda b, n: (b * nb + n, 0)),
        scratch_shapes=[pltpu.VMEM((WINDOW, q_cols), F32)],
        compiler_params=_params("parallel", "arbitrary"),
        name="swa_attention",
    )(sinks_l, q, kv, kv, gain)


def _conv_kernel(xc_ref, xh_ref, w_ref, cb_ref, lg_ref, lb_ref, og_ref, o_ref,
                 xp_ref, xs_ref, y_ref, *, tt, rows):
    t = pl.program_id(1)
    ch = xc_ref.shape[1]
    xp_ref[0:HALO, :] = jnp.where(t > 0, xh_ref[...], 0.0)
    xp_ref[HALO:HALO + tt, :] = xc_ref[...]
    first = HALO - (CONV_WIDTH - 1)
    span = xs_ref.shape[1]

    def lane_chunk(ci, carry):
        lanes = pl.ds(pl.multiple_of(ci * LANES, LANES), LANES)
        wts = w_ref[:, lanes]
        cb = cb_ref[:, lanes]
        for r in range(1, SUBLANES):
            xs_ref[r - 1] = xp_ref[pl.ds(r, span), lanes]
        for r0 in range(0, tt, rows):
            acc = jnp.zeros((rows, LANES), F32)
            for k in range(CONV_WIDTH):
                q, r = divmod(first + k, SUBLANES)
                lo = SUBLANES * q + r0
                if r == 0:
                    xv = xp_ref[lo:lo + rows, lanes]
                else:
                    xv = xs_ref[r - 1, lo:lo + rows, :]
                acc = acc + xv * wts[k:k + 1, :]
            y_ref[r0:r0 + rows, lanes] = acc + cb
        return carry

    lax.fori_loop(0, ch // LANES, lane_chunk, 0)

    y = y_ref[...]
    mu = jnp.mean(y, axis=-1, keepdims=True)
    dlt = y - mu
    var = jnp.mean(dlt * dlt, axis=-1, keepdims=True)
    z = dlt * lax.rsqrt(var + 1e-5) * lg_ref[...] + lb_ref[...]
    s = z * jax.nn.sigmoid(z)
    ms = jnp.mean(s * s, axis=-1, keepdims=True)
    o_ref[...] = (s * lax.rsqrt(ms + 1e-6) * og_ref[...]).astype(BF16)


def _conv_module(glu, conv_w, conv_b, ln_g, ln_b, out_g, l, seq, tt=256, rows=64):
    m, ch = glu.shape
    nt = seq // tt
    hb = tt // HALO
    vec = pl.BlockSpec((None, 1, ch), lambda b, t: (l, 0, 0))
    return pl.pallas_call(
        functools.partial(_conv_kernel, tt=tt, rows=rows),
        out_shape=jax.ShapeDtypeStruct((m, ch), BF16),
        grid=(m // seq, nt),
        in_specs=[
            pl.BlockSpec((tt, ch), lambda b, t: (b * nt + t, 0)),
            pl.BlockSpec((HALO, ch), lambda b, t: (jnp.maximum((b * nt + t) * hb - 1, 0), 0)),
            pl.BlockSpec((None, CONV_WIDTH, ch), lambda b, t: (l, 0, 0)),
            vec, vec, vec, vec,
        ],
        out_specs=pl.BlockSpec((tt, ch), lambda b, t: (b * nt + t, 0)),
        scratch_shapes=[pltpu.VMEM((HALO + tt, ch), F32),
                        pltpu.VMEM((SUBLANES - 1, HALO + tt - SUBLANES, LANES), F32),
                        pltpu.VMEM((tt, ch), F32)],
        compiler_params=_params("parallel", "arbitrary"),
        name="conv_module",
    )(glu, glu, conv_w, conv_b, ln_g, ln_b, out_g)


def _out_proj_kernel(a_ref, c_ref, w_ref, g_ref, x_ref, o_ref):
    ka = a_ref.shape[1]
    w = w_ref[...].astype(BF16)
    acc = (jnp.dot(a_ref[...], w[:ka], preferred_element_type=F32)
           + jnp.dot(c_ref[...], w[ka:], preferred_element_type=F32))
    o_ref[...] = x_ref[...] + g_ref[...] * acc


def _out_proj(attn, conv, w_out, mod5, l, x2, seq, tm, tn):
    m, ka = attn.shape
    kc = conv.shape[1]
    d = x2.shape[1]
    per_b = seq // tm
    return pl.pallas_call(
        _out_proj_kernel,
        out_shape=jax.ShapeDtypeStruct((m, d), F32),
        grid=(m // tm, d // tn),
        in_specs=[
            pl.BlockSpec((tm, ka), lambda i, j: (i, 0), pipeline_mode=pl.Buffered(1)),
            pl.BlockSpec((tm, kc), lambda i, j: (i, 0), pipeline_mode=pl.Buffered(1)),
            pl.BlockSpec((None, ka + kc, tn), lambda i, j: (l, 0, j)),
            pl.BlockSpec((None, None, None, 1, tn), lambda i, j: (l, i // per_b, 2, 0, j)),
            pl.BlockSpec((tm, tn), lambda i, j: (i, j)),
        ],
        out_specs=pl.BlockSpec((tm, tn), lambda i, j: (i, j)),
        compiler_params=_params("parallel", "arbitrary"),
        name="out_proj",
    )(attn, conv, w_out, mod5, x2)


def _down_kernel(a_ref, w_ref, g_ref, x_ref, o_ref):
    acc = jnp.dot(a_ref[...], w_ref[...].astype(BF16), preferred_element_type=F32)
    o_ref[...] = x_ref[...] + g_ref[...] * acc


def _down_proj(act, w_down, mod5, l, x2, seq, tm, tn):
    m, f = act.shape
    d = x2.shape[1]
    per_b = seq // tm
    return pl.pallas_call(
        _down_kernel,
        out_shape=jax.ShapeDtypeStruct((m, d), F32),
        grid=(m // tm, d // tn),
        in_specs=[
            pl.BlockSpec((tm, f), lambda i, j: (i, 0), pipeline_mode=pl.Buffered(1)),
            pl.BlockSpec((None, f, tn), lambda i, j: (l, 0, j)),
            pl.BlockSpec((None, None, None, 1, tn), lambda i, j: (l, i // per_b, 5, 0, j)),
            pl.BlockSpec((tm, tn), lambda i, j: (i, j)),
        ],
        out_specs=pl.BlockSpec((tm, tn), lambda i, j: (i, j)),
        compiler_params=pltpu.CompilerParams(
            dimension_semantics=("parallel", "arbitrary"),
            vmem_limit_bytes=DOWN_VMEM_LIMIT),
        name="ffn_down",
    )(act, w_down, mod5, x2)


def kernel(x, c, positions, w_ada, b_ada, norm1_g, w_in, q_norm_g, k_norm_g, sinks,
           conv_w, conv_b, conv_ln_g, conv_ln_b, attn_out_g, conv_out_g, w_out,
           norm2_g, w_ffn_gate, w_ffn_up, w_ffn_down):
    bsz, seq, d = x.shape
    depth = w_ada.shape[0]
    q_cols = attn_out_g.shape[1]
    conv_ch = conv_out_g.shape[1]
    kv_cols = (w_in.shape[2] - q_cols - 2 * conv_ch) // 2
    d_ff = w_ffn_gate.shape[2]
    m = bsz * seq

    c8 = jnp.zeros((8, d), F32).at[:bsz].set(c)
    mod = _ada_mod(c8, w_ada, b_ada)
    mod5 = mod.reshape(depth, 8, N_MOD, 1, d)
    tabs = _rope_tables(positions)

    def vec3(p):
        return p.reshape(depth, 1, p.shape[-1])

    qg = vec3(jnp.tile(q_norm_g, (1, LANES // HEAD_DIM)))
    kg = vec3(jnp.tile(k_norm_g, (1, LANES // HEAD_DIM)))
    n1, n2 = vec3(norm1_g), vec3(norm2_g)
    cw = conv_w.reshape(depth, CONV_WIDTH, conv_ch)
    cb, lg, lb = vec3(conv_b), vec3(conv_ln_g), vec3(conv_ln_b)
    ag, og = vec3(attn_out_g), vec3(conv_out_g)

    x2 = x.reshape(m, d)
    u_off = q_cols + 2 * kv_cols
    for l in range(depth):
        h = _norm_mod(x2, n1, mod5, l, 1, 0, seq)
        q = _head_proj(h, w_in, l, qg, tabs, 0, q_cols, None, HEAD_DIM ** -0.5,
                       "q_proj", tm=TM_WIDE)
        kv = _head_proj(h, w_in, l, kg, tabs, q_cols, 2 * kv_cols, kv_cols, 1.0,
                        "kv_proj", tm=TM_WIDE)
        glu = _pair_proj(h, w_in, w_in, l, u_off, u_off + conv_ch, conv_ch, F32,
                         False, "glu_proj", tm=TM_WIDE)
        attn = _attention(q, kv, sinks[l].reshape(1, -1), ag, l, seq)
        conv = _conv_module(glu, cw, cb, lg, lb, og, l, seq)
        x2 = _out_proj(attn, conv, w_out, mod5, l, x2, seq, tm=TM_WIDE, tn=256)
        h = _norm_mod(x2, n2, mod5, l, 4, 3, seq)
        act = _pair_proj(h, w_ffn_gate, w_ffn_up, l, 0, 0, d_ff, BF16, True,
                         "ffn_gate_up", tm=TM_WIDE)
        x2 = _down_proj(act, w_ffn_down, mod5, l, x2, seq, tm=1024, tn=256)
    return x2.reshape(bsz, seq, d)
```
